```python
import jax, jax.numpy as jnp
from jax import lax
import numpy as np

D_MODEL = 1024
BATCH = 8
SEQ = 2048
DEPTH = 2
DEC_BATCH = 32
DEC_SEQ = 4
PAST_LEN = 8192
PAGE_SIZE = 128

GROUP_WIDTH = D_MODEL // 2
MIX_WIDTH = 4 * GROUP_WIDTH
CONV_W = 4
GDN_HEADS = 4
GDN_DK = GROUP_WIDTH // GDN_HEADS
GDN_DV = GDN_DK
GDN_CHUNK = 64
SSD_HEADS = 8
SSD_P = GROUP_WIDTH // SSD_HEADS
SSD_GROUPS = 2
SSD_N = 64
SSD_CHUNK = 64
FOX_HEADS = 8
FOX_DH = GROUP_WIDTH // FOX_HEADS
FOX_BLOCK = 128
MEM_LEN = 256
MEM_HEADS = 4
MEM_DH = GROUP_WIDTH // MEM_HEADS

GDN_QKV = 3 * GROUP_WIDTH
SSD_XBC = GROUP_WIDTH + 2 * SSD_GROUPS * SSD_N
FOX_QKV = 3 * GROUP_WIDTH
IN_SPLITS = (GDN_QKV, GDN_HEADS, GDN_HEADS, GROUP_WIDTH,
             SSD_XBC, SSD_HEADS, GROUP_WIDTH,
             FOX_QKV, FOX_HEADS, GROUP_WIDTH,
             GROUP_WIDTH, GROUP_WIDTH)
N_IN = sum(IN_SPLITS)
DEEPNORM_ALPHA = (2 * DEPTH) ** 0.25
DEEPNORM_BETA = (8 * DEPTH) ** -0.25
EPS = 1e-6
NEG_BIG = -1e30

kernel_name = "hymba_gdn_ssd_fox_memory_deepnorm_step"


def _split_cols(t, sizes):
    idx = np.cumsum(sizes)[:-1].tolist()
    return jnp.split(t, idx, axis=-1)


def _layer_norm(x, g, b):
    xf = x.astype(jnp.float32)
    mu = jnp.mean(xf, -1, keepdims=True)
    var = jnp.mean(jnp.square(xf - mu), -1, keepdims=True)
    return ((xf - mu) * lax.rsqrt(var + EPS) * g + b).astype(x.dtype)


def _rms_norm(x, g):
    xf = x.astype(jnp.float32)
    return xf * lax.rsqrt(jnp.mean(jnp.square(xf), -1, keepdims=True) + EPS) * g


def _l2norm(x):
    xf = x.astype(jnp.float32)
    return xf * lax.rsqrt(jnp.sum(jnp.square(xf), -1, keepdims=True) + EPS)


def _causal_conv(u, buf, w, b):
    L = u.shape[1]
    cat = jnp.concatenate([buf.astype(u.dtype), u], axis=1)
    y = b + sum(cat[:, i:i + L] * w[i] for i in range(CONV_W))
    return jax.nn.silu(y), cat[:, -(CONV_W - 1):]


def _to_chunks(t, c):
    B, L = t.shape[:2]
    pad = (-L) % c
    t = jnp.pad(t, [(0, 0), (0, pad)] + [(0, 0)] * (t.ndim - 2))
    t = t.reshape((B, (L + pad) // c, c) + t.shape[2:])
    return jnp.moveaxis(t, 2, 3)


def _from_chunks(t, L):
    N, B, H, C, D = t.shape
    return t.transpose(1, 0, 3, 2, 4).reshape(B, N * C, H, D)[:, :L]


def _decay_matrix(gam):
    C = gam.shape[-1]
    tril = jnp.tril(jnp.ones((C, C), dtype=bool))
    diff = gam[..., :, None] - gam[..., None, :]
    return jnp.where(tril, jnp.exp(jnp.where(tril, diff, 0.0)), 0.0)


def _gated_delta_rule(q, k, v, g, beta, S0):
    f32 = jnp.float32
    L = q.shape[1]
    c = min(GDN_CHUNK, L)
    qc, kc, vc = (_to_chunks(t.astype(f32), c) for t in (q, k, v))
    gc, bc = (_to_chunks(t.astype(f32), c) for t in (g, beta))
    gam = jnp.cumsum(gc, axis=-1)
    decay = _decay_matrix(gam)
    strict = jnp.tril(jnp.ones((c, c), dtype=bool), -1)
    a = jnp.where(strict, bc[..., :, None] * jnp.einsum('bnhid,bnhjd->bnhij', kc, kc) * decay, 0.0)
    rhs = jnp.concatenate([vc * bc[..., None], kc * (bc * jnp.exp(gam))[..., None]], axis=-1)
    sol = lax.linalg.triangular_solve(jnp.eye(c, dtype=f32) + a, rhs, left_side=True, lower=True)
    dv = v.shape[-1]
    w_val, w_key = sol[..., :dv], sol[..., dv:]
    qk = jnp.einsum('bnhid,bnhjd->bnhij', qc, kc) * decay
    q_dec = qc * jnp.exp(gam)[..., None]
    k_tail = kc * jnp.exp(gam[..., -1:] - gam)[..., None]
    chunk_dec = jnp.exp(gam[..., -1])

    def step(S, xs):
        wv, wk, qk_n, qd, kt, cd = xs
        u = wv - jnp.einsum('bhck,bhkv->bhcv', wk, S)
        o = jnp.einsum('bhck,bhkv->bhcv', qd, S) + jnp.einsum('bhij,bhjv->bhiv', qk_n, u)
        S = S * cd[..., None, None] + jnp.einsum('bhck,bhcv->bhkv', kt, u)
        return S, o

    xs = tuple(jnp.moveaxis(t, 1, 0) for t in (w_val, w_key, qk, q_dec, k_tail, chunk_dec))
    S, o = lax.scan(step, S0.astype(f32), xs)
    return _from_chunks(o, L), S


def _ssd_scan(x, dt, A, Bm, Cm, h0):
    f32 = jnp.float32
    L, H, G = x.shape[1], x.shape[2], Bm.shape[2]
    c = min(SSD_CHUNK, L)
    Bh = jnp.repeat(Bm.astype(f32), H // G, axis=2)
    Ch = jnp.repeat(Cm.astype(f32), H // G, axis=2)
    xdt = x.astype(f32) * dt[..., None]
    xc, bc, cc = (_to_chunks(t, c) for t in (xdt, Bh, Ch))
    gam = jnp.cumsum(_to_chunks(dt * A, c), axis=-1)
    decay = _decay_matrix(gam)
    cb = jnp.einsum('bnhis,bnhjs->bnhij', cc, bc) * decay
    y_intra = jnp.einsum('bnhij,bnhjp->bnhip', cb, xc)
    c_dec = cc * jnp.exp(gam)[..., None]
    b_tail = bc * jnp.exp(gam[..., -1:] - gam)[..., None]
    chunk_dec = jnp.exp(gam[..., -1])

    def step(h, xs):
        yi, cdn, bt, xn, dec = xs
        y = yi + jnp.einsum('bhcs,bhps->bhcp', cdn, h)
        h = h * dec[..., None, None] + jnp.einsum('bhcp,bhcs->bhps', xn, bt)
        return h, y

    xs = tuple(jnp.moveaxis(t, 1, 0) for t in (y_intra, c_dec, b_tail, xc, chunk_dec))
    h, y = lax.scan(step, h0.astype(f32), xs)
    return _from_chunks(y, L), h


def _fox_block(q, k, v, cq, ck, q_pos, k_pos):
    s = jnp.einsum('bqhd,bkhd->bhqk', q, k).astype(jnp.float32) * FOX_DH ** -0.5
    s = s + jnp.swapaxes(cq, 1, 2)[..., :, None] - jnp.swapaxes(ck, 1, 2)[..., None, :]
    s = jnp.where(k_pos[None, :] <= q_pos[:, None], s, NEG_BIG)
    p = jax.nn.softmax(s, axis=-1)
    return jnp.einsum('bhqk,bkhd->bqhd', p.astype(v.dtype), v)


def _forgetting_attention(q, k, v, cq, ck, q_pos, k_pos):
    B, L, H, D = q.shape
    qb = min(FOX_BLOCK, L)
    pad = (-L) % qb
    nb = (L + pad) // qb

    def blocks(t):
        t = jnp.pad(t, [(0, 0), (0, pad)] + [(0, 0)] * (t.ndim - 2))
        return jnp.moveaxis(t.reshape((B, nb, qb) + t.shape[2:]), 1, 0)

    pos_b = jnp.pad(q_pos, (0, pad), mode='edge').reshape(nb, qb)
    out = lax.map(lambda t: _fox_block(t[0], k, v, t[1], ck, t[2], k_pos), (blocks(q), blocks(cq), pos_b))
    return jnp.moveaxis(out, 0, 1).reshape(B, nb * qb, H, D)[:, :L]


def _memory_attention(q, mk, mv):
    s = jnp.einsum('blhd,bmhd->bhlm', q, mk).astype(jnp.float32) * MEM_DH ** -0.5
    p = jax.nn.softmax(s, axis=-1)
    return jnp.einsum('bhlm,bmhd->blhd', p.astype(mv.dtype), mv)


def _hybrid_layer(x, mem_k, mem_v, gdn_conv0, gdn_s0, ssd_conv0, ssd_h0,
                  fox_k_past, fox_v_past, fox_logf_past, p):
    f32 = jnp.float32
    B, L, _ = x.shape
    past = 0 if fox_k_past is None else fox_k_past.shape[1]
    proj = jnp.einsum('bld,de->ble', x, p['w_in'])
    (a_qkv, a_beta, a_alpha, a_gate, b_xbc, b_dt, b_gate,
     c_qkv, c_f, c_gate, d_q, d_gate) = _split_cols(proj, IN_SPLITS)

    a_qkv, gdn_conv_new = _causal_conv(a_qkv, gdn_conv0, p['gdn_conv_w'], p['gdn_conv_b'])
    qa, ka, va = (t.reshape(B, L, GDN_HEADS, GDN_DK) for t in jnp.split(a_qkv, 3, axis=-1))
    beta = jax.nn.sigmoid(a_beta.astype(f32))
    g = -jnp.exp(p['gdn_a_log'].astype(f32)) * jax.nn.softplus(a_alpha.astype(f32) + p['gdn_dt_bias'])
    o_a, gdn_s_new = _gated_delta_rule(_l2norm(qa) * GDN_DK ** -0.5, _l2norm(ka), va, g, beta, gdn_s0)
    y_a = _rms_norm(o_a, p['gdn_norm_w']).reshape(B, L, GROUP_WIDTH).astype(x.dtype) * jax.nn.silu(a_gate)

    b_xbc, ssd_conv_new = _causal_conv(b_xbc, ssd_conv0, p['ssd_conv_w'], p['ssd_conv_b'])
    xs, bm, cm = _split_cols(b_xbc, (GROUP_WIDTH, SSD_GROUPS * SSD_N, SSD_GROUPS * SSD_N))
    xs = xs.reshape(B, L, SSD_HEADS, SSD_P)
    dt = jax.nn.softplus(b_dt.astype(f32) + p['ssd_dt_bias'])
    y_b, ssd_h_new = _ssd_scan(xs, dt, -jnp.exp(p['ssd_a_log'].astype(f32)),
                               bm.reshape(B, L, SSD_GROUPS, SSD_N), cm.reshape(B, L, SSD_GROUPS, SSD_N), ssd_h0)
    y_b = y_b + p['ssd_d'][:, None] * xs
    y_b = _rms_norm(y_b.reshape(B, L, GROUP_WIDTH) * jax.nn.silu(b_gate.astype(f32)), p['ssd_norm_w']).astype(x.dtype)

    qc, kc, vc = (t.reshape(B, L, FOX_HEADS, FOX_DH) for t in jnp.split(c_qkv, 3, axis=-1))
    logf = jax.nn.log_sigmoid(c_f.astype(f32) + p['fox_f_bias'])
    if fox_k_past is None:
        k_all, v_all, logf_all = kc, vc, logf
    else:
        k_all = jnp.concatenate([fox_k_past.astype(kc.dtype), kc], axis=1)
        v_all = jnp.concatenate([fox_v_past.astype(vc.dtype), vc], axis=1)
        logf_all = jnp.concatenate([fox_logf_past.astype(f32), logf], axis=1)
    c_all = jnp.cumsum(logf_all, axis=1)
    o_c = _forgetting_attention(qc, k_all, v_all, c_all[:, past:], c_all,
                                past + jnp.arange(L), jnp.arange(past + L))
    y_c = o_c.reshape(B, L, GROUP_WIDTH) * jax.nn.silu(c_gate)

    o_d = _memory_attention(d_q.reshape(B, L, MEM_HEADS, MEM_DH), mem_k, mem_v)
    y_d = o_d.reshape(B, L, GROUP_WIDTH) * jax.nn.silu(d_gate)

    mixed = jnp.concatenate([y_a, y_b, y_c, y_d], axis=-1)
    y = jnp.einsum('ble,ed->bld', mixed, p['w_out'])
    x_new = _layer_norm(DEEPNORM_ALPHA * x + y, p['ln_g'], p['ln_b'])
    return (x_new, gdn_conv_new, gdn_s_new.astype(gdn_s0.dtype), ssd_conv_new,
            ssd_h_new.astype(ssd_h0.dtype), kc, vc, logf.astype(x.dtype))


def setup_inputs(seed: int = 0) -> dict:
    key = jax.random.key(seed)
    ks = iter(jax.random.split(key, 48))
    f32 = jnp.float32

    def nrm(shape, scale=1.0):
        return scale * jax.random.normal(next(ks), shape, f32)

    def log_uniform_a(shape):
        return jnp.log(jax.random.uniform(next(ks), shape, f32, 1.0, 16.0))

    def dt_bias(shape):
        dt = jnp.exp(jax.random.uniform(next(ks), shape, f32, np.log(1e-3), np.log(1e-1)))
        return dt + jnp.log(-jnp.expm1(-dt))

    n_pages = PAST_LEN // PAGE_SIZE
    n_used = DEC_BATCH * n_pages
    n_pool = n_used + n_used // 4
    page_table = jax.random.permutation(next(ks), n_pool)[:n_used].reshape(DEC_BATCH, n_pages).astype(jnp.int32)

    return {
        "x_prompt": nrm((BATCH, SEQ, D_MODEL)),
        "x_sample": nrm((DEC_BATCH, DEC_SEQ, D_MODEL)),
        "state_gdn_conv": nrm((DEPTH, DEC_BATCH, CONV_W - 1, GDN_QKV)),
        "state_gdn": nrm((DEPTH, DEC_BATCH, GDN_HEADS, GDN_DK, GDN_DV), GDN_DK ** -0.5),
        "state_ssd_conv": nrm((DEPTH, DEC_BATCH, CONV_W - 1, SSD_XBC)),
        "state_ssd": nrm((DEPTH, DEC_BATCH, SSD_HEADS, SSD_P, SSD_N), 0.5),
        "cache_fox_k": nrm((DEPTH, n_pool, PAGE_SIZE, FOX_HEADS, FOX_DH)),
        "cache_fox_v": nrm((DEPTH, n_pool, PAGE_SIZE, FOX_HEADS, FOX_DH)),
        "cache_fox_logf": jax.nn.log_sigmoid(3.0 + nrm((DEPTH, n_pool, PAGE_SIZE, FOX_HEADS))),
        "cache_mem_k": nrm((DEPTH, DEC_BATCH, MEM_LEN, MEM_HEADS, MEM_DH)),
        "cache_mem_v": nrm((DEPTH, DEC_BATCH, MEM_LEN, MEM_HEADS, MEM_DH)),
        "page_table": page_table,
        "mem_prompt": nrm((BATCH, MEM_LEN, D_MODEL)),
        "w_in": nrm((DEPTH, D_MODEL, N_IN), D_MODEL ** -0.5),
        "gdn_conv_w": nrm((DEPTH, CONV_W, GDN_QKV), CONV_W ** -0.5),
        "gdn_conv_b": nrm((DEPTH, GDN_QKV), 0.02),
        "gdn_a_log": log_uniform_a((DEPTH, GDN_HEADS)),
        "gdn_dt_bias": dt_bias((DEPTH, GDN_HEADS)),
        "gdn_norm_w": 1.0 + nrm((DEPTH, GDN_DV), 0.02),
        "ssd_conv_w": nrm((DEPTH, CONV_W, SSD_XBC), CONV_W ** -0.5),
        "ssd_conv_b": nrm((DEPTH, SSD_XBC), 0.02),
        "ssd_a_log": log_uniform_a((DEPTH, SSD_HEADS)),
        "ssd_dt_bias": dt_bias((DEPTH, SSD_HEADS)),
        "ssd_d": 1.0 + nrm((DEPTH, SSD_HEADS), 0.1),
        "ssd_norm_w": 1.0 + nrm((DEPTH, GROUP_WIDTH), 0.02),
        "fox_f_bias": 3.0 + nrm((DEPTH, FOX_HEADS), 0.1),
        "w_mem_kv": nrm((DEPTH, D_MODEL, 2 * GROUP_WIDTH), D_MODEL ** -0.5),
        "w_out": nrm((DEPTH, MIX_WIDTH, D_MODEL), DEEPNORM_BETA * MIX_WIDTH ** -0.5),
        "ln_g": 1.0 + nrm((DEPTH, D_MODEL), 0.02),
        "ln_b": nrm((DEPTH, D_MODEL), 0.02),
    }


def reference(x_prompt, x_sample, state_gdn_conv, state_gdn, state_ssd_conv, state_ssd,
              cache_fox_k, cache_fox_v, cache_fox_logf, cache_mem_k, cache_mem_v, page_table,
              mem_prompt, w_in, gdn_conv_w, gdn_conv_b, gdn_a_log, gdn_dt_bias, gdn_norm_w,
              ssd_conv_w, ssd_conv_b, ssd_a_log, ssd_dt_bias, ssd_d, ssd_norm_w, fox_f_bias,
              w_mem_kv, w_out, ln_g, ln_b):
    dt_ = x_prompt.dtype
    bp, lp = x_prompt.shape[:2]
    bs = x_sample.shape[0]
    past_len = page_table.shape[1] * cache_fox_k.shape[2]
    names = ('gdn_conv', 'gdn_state', 'ssd_conv', 'ssd_state', 'fox_k', 'fox_v', 'fox_logf')
    pn = {n: [] for n in names + ('mem_k', 'mem_v')}
    sn = {n: [] for n in names}
    xp, xs = x_prompt, x_sample
    for l in range(DEPTH):
        p = dict(w_in=w_in[l], gdn_conv_w=gdn_conv_w[l], gdn_conv_b=gdn_conv_b[l],
                 gdn_a_log=gdn_a_log[l], gdn_dt_bias=gdn_dt_bias[l], gdn_norm_w=gdn_norm_w[l],
                 ssd_conv_w=ssd_conv_w[l], ssd_conv_b=ssd_conv_b[l], ssd_a_log=ssd_a_log[l],
                 ssd_dt_bias=ssd_dt_bias[l], ssd_d=ssd_d[l], ssd_norm_w=ssd_norm_w[l],
                 fox_f_bias=fox_f_bias[l], w_out=w_out[l], ln_g=ln_g[l], ln_b=ln_b[l])
        mkv = jnp.einsum('bmd,de->bme', mem_prompt, w_mem_kv[l])
        mk, mv = (t.reshape(bp, mem_prompt.shape[1], MEM_HEADS, MEM_DH) for t in jnp.split(mkv, 2, axis=-1))
        out_p = _hybrid_layer(
            xp, mk, mv,
            jnp.zeros((bp, CONV_W - 1, GDN_QKV), dt_), jnp.zeros((bp, GDN_HEADS, GDN_DK, GDN_DV), dt_),
            jnp.zeros((bp, CONV_W - 1, SSD_XBC), dt_), jnp.zeros((bp, SSD_HEADS, SSD_P, SSD_N), dt_),
            None, None, None, p)
        xp = out_p[0]
        for n, t in zip(names, out_p[1:]):
            pn[n].append(t)
        pn['mem_k'].append(mk)
        pn['mem_v'].append(mv)
        fk = cache_fox_k[l][page_table].reshape(bs, past_len, FOX_HEADS, FOX_DH)
        fv = cache_fox_v[l][page_table].reshape(bs, past_len, FOX_HEADS, FOX_DH)
        fl = cache_fox_logf[l][page_table].reshape(bs, past_len, FOX_HEADS)
        out_s = _hybrid_layer(
            xs, cache_mem_k[l], cache_mem_v[l],
            state_gdn_conv[l], state_gdn[l], state_ssd_conv[l], state_ssd[l],
            fk, fv, fl, p)
        xs = out_s[0]
        for n, t in zip(names, out_s[1:]):
            sn[n].append(t)
    P = {n: jnp.stack(v) for n, v in pn.items()}
    S = {n: jnp.stack(v) for n, v in sn.items()}
    return (xp, xs,
            P['gdn_conv'], P['gdn_state'], P['ssd_conv'], P['ssd_state'],
            P['fox_k'], P['fox_v'], P['fox_logf'], P['mem_k'], P['mem_v'],
            S['gdn_conv'], S['gdn_state'], S['ssd_conv'], S['ssd_state'],
            S['fox_k'], S['fox_v'], S['fox_logf'])
```

```python
import functools

import numpy as np
import jax
import jax.numpy as jnp
from jax import lax
from jax.experimental import pallas as pl
from jax.experimental.pallas import tpu as pltpu

F32 = jnp.float32
BF16 = jnp.bfloat16

D_MODEL = 1024
GROUP_WIDTH = 512
MIX_WIDTH = 2048
CONV_W = 4
GDN_HEADS = 4
GDN_DK = 128
SSD_HEADS = 8
SSD_P = 64
SSD_GROUPS = 2
SSD_N = 64
FOX_HEADS = 8
FOX_DH = 64
MEM_HEADS = 4
MEM_DH = 128
CHUNK = 64
GDN_QKV = 3 * GROUP_WIDTH
SSD_XBC = GROUP_WIDTH + 2 * SSD_GROUPS * SSD_N
IN_SPLITS = (GDN_QKV, GDN_HEADS, GDN_HEADS, GROUP_WIDTH,
             SSD_XBC, SSD_HEADS, GROUP_WIDTH,
             3 * GROUP_WIDTH, FOX_HEADS, GROUP_WIDTH,
             GROUP_WIDTH, GROUP_WIDTH)
EPS = 1e-6
NEG_BIG = -1e30

LANES = 128
SUBLANES = 8
SAMPLE_ROWS = 8
INV_BLOCK = 16
VMEM_LIMIT = 48 * 1024 * 1024

AUX_BETA = 0
AUX_G = 4
AUX_DT = 8
AUX_LOGF = 16
AUX_C = 24
AUX_DTA = 32

PACK_WIDTHS = (GDN_QKV, GROUP_WIDTH, SSD_XBC, GROUP_WIDTH, GROUP_WIDTH, GROUP_WIDTH, GROUP_WIDTH,
               GROUP_WIDTH, GROUP_WIDTH, GROUP_WIDTH)
PACK_TOTAL = sum(PACK_WIDTHS) + LANES


def _dot(a, b):
    return jnp.dot(a, b, preferred_element_type=F32)


def _dot_nt(a, b):
    return lax.dot_general(a, b, (((1,), (1,)), ((), ())), preferred_element_type=F32)


def _dot_tn(a, b):
    return lax.dot_general(a, b, (((0,), (0,)), ((), ())), preferred_element_type=F32)


def _split2(a):
    hi = a.astype(BF16)
    lo = (a - hi.astype(F32)).astype(BF16)
    return hi, lo


def _split3(a):
    a1 = a.astype(BF16)
    r = a - a1.astype(F32)
    a2 = r.astype(BF16)
    a3 = (r - a2.astype(F32)).astype(BF16)
    return a1, a2, a3


def _dot3(a, b):
    ah, al = _split2(a)
    bh, bl = _split2(b)
    return _dot(ah, bh) + (_dot(ah, bl) + _dot(al, bh))


def _dot_exact_lhs(t, a):
    a1, a2, a3 = _split3(a)
    return _dot(t, a1) + (_dot(t, a2) + _dot(t, a3))


def _dot_exact_rhs(a, t):
    a1, a2, a3 = _split3(a)
    return _dot(a1, t) + (_dot(a2, t) + _dot(a3, t))


def _sigmoid(x):
    return 1.0 / (1.0 + jnp.exp(-x))


def _silu(x):
    return x * _sigmoid(x)


def _iota(shape, dim):
    return lax.broadcasted_iota(jnp.int32, shape, dim)


def _tril_ones(n, seg=None):
    r = _iota((n, n), 0)
    c = _iota((n, n), 1)
    m = c <= r
    if seg is not None:
        m = jnp.logical_and(m, (r // seg) == (c // seg))
    return jnp.where(m, 1.0, 0.0).astype(BF16)


def _params(*sem):
    return pltpu.CompilerParams(dimension_semantics=sem, vmem_limit_bytes=VMEM_LIMIT)


def _inproj_kernel(x_ref, w_ref, p_ref, *refs, tm, seq_rows):
    out_refs = refs[:len(PACK_WIDTHS)]
    aux_ref = refs[len(PACK_WIDTHS)]
    carry_ref = refs[len(PACK_WIDTHS) + 1]
    xb = x_ref[...].astype(BF16)
    off = 0
    for ref, width in zip(out_refs, PACK_WIDTHS):
        for c0 in range(0, width, GROUP_WIDTH):
            cw = min(GROUP_WIDTH, width - c0)
            ref[:, c0:c0 + cw] = _dot(xb, w_ref[:, off + c0:off + c0 + cw])
        off += width
    raw = _dot(xb, w_ref[:, off:off + LANES])
    z = raw + p_ref[0:1, :]
    e = jnp.exp(-jnp.abs(z))
    l1p = jnp.log1p(e)
    softplus = jnp.maximum(z, 0.0) + l1p
    log_sigmoid = -(jnp.maximum(-z, 0.0) + l1p)
    neg_exp = -jnp.exp(p_ref[1:2, :])
    lane = _iota((tm, LANES), 1)
    aux = jnp.where(lane < AUX_G, _sigmoid(z),
          jnp.where(lane < AUX_DT, neg_exp * softplus,
          jnp.where(lane < AUX_LOGF, softplus,
          jnp.where(lane < AUX_DTA, log_sigmoid,
          jnp.where(lane < AUX_DTA + SSD_HEADS, softplus * neg_exp, 0.0)))))
    if seq_rows <= tm:
        cum = _dot_exact_lhs(_tril_ones(tm, seq_rows), aux)
    else:
        i = pl.program_id(0)

        @pl.when((i * tm) % seq_rows == 0)
        def _():
            carry_ref[...] = jnp.zeros_like(carry_ref)

        cum = _dot_exact_lhs(_tril_ones(tm), aux) + carry_ref[0:1, :]
        carry_ref[...] = jnp.broadcast_to(cum[tm - 1:tm, :], carry_ref.shape)
    in_c = jnp.logical_and(lane >= AUX_C, lane < AUX_DTA)
    aux_ref[...] = jnp.where(in_c, cum, aux)


def _inproj(x2d, w_packed, pvec, *, tm, seq_rows):
    m = x2d.shape[0]
    assert m % tm == 0 and (seq_rows <= tm and tm % seq_rows == 0 or seq_rows % tm == 0)
    out_shape = [jax.ShapeDtypeStruct((m, w), F32) for w in PACK_WIDTHS] + [jax.ShapeDtypeStruct((m, LANES), F32)]
    out_specs = [pl.BlockSpec((tm, w), lambda i: (i, 0)) for w in PACK_WIDTHS] + [pl.BlockSpec((tm, LANES), lambda i: (i, 0))]
    return pl.pallas_call(
        functools.partial(_inproj_kernel, tm=tm, seq_rows=seq_rows),
        grid=(m // tm,),
        in_specs=[pl.BlockSpec((tm, D_MODEL), lambda i: (i, 0)),
                  pl.BlockSpec((D_MODEL, PACK_TOTAL), lambda i: (0, 0), pipeline_mode=pl.Buffered(1)),
                  pl.BlockSpec((SUBLANES, LANES), lambda i: (0, 0))],
        out_specs=out_specs,
        out_shape=out_shape,
        scratch_shapes=[pltpu.VMEM((SUBLANES, LANES), F32)],
        compiler_params=_params("arbitrary"),
        name="inproj",
    )(x2d, w_packed, pvec)


def _pack_inproj_weights(w_in_l, gdn_a_log, gdn_dt_bias, ssd_a_log, ssd_dt_bias, fox_f_bias):
    offs = np.concatenate([[0], np.cumsum(IN_SPLITS)])
    seg = lambda k: w_in_l[:, offs[k]:offs[k + 1]]
    small = jnp.concatenate([seg(1), seg(2), seg(5), seg(8), seg(8), seg(5)], axis=1)
    small = jnp.pad(small, ((0, 0), (0, LANES - small.shape[1])))
    c_qkv = seg(7)
    cols = [seg(0), seg(3), seg(4), seg(6), c_qkv[:, :GROUP_WIDTH], c_qkv[:, GROUP_WIDTH:2 * GROUP_WIDTH],
            c_qkv[:, 2 * GROUP_WIDTH:], seg(9), seg(10), seg(11), small]
    w_packed = jnp.concatenate(cols, axis=1).astype(BF16)
    z4 = jnp.zeros((4,), F32)
    z8 = jnp.zeros((8,), F32)
    tail = jnp.zeros((LANES - AUX_DTA - SSD_HEADS,), F32)
    bias_row = jnp.concatenate([z4, gdn_dt_bias, ssd_dt_bias, fox_f_bias, fox_f_bias, ssd_dt_bias, tail])
    alog_row = jnp.concatenate([z4, gdn_a_log, z8, z8, z8, ssd_a_log, tail])
    pvec = jnp.zeros((SUBLANES, LANES), F32).at[0].set(bias_row).at[1].set(alog_row)
    return w_packed, pvec


def _memkv_kernel(x_ref, w_ref, k_ref, v_ref):
    xb = x_ref[...].astype(BF16)
    k_ref[...] = _dot(xb, w_ref[:, :GROUP_WIDTH])
    v_ref[...] = _dot(xb, w_ref[:, GROUP_WIDTH:])


def _memkv(mem2d, w_bf16, *, tm=256):
    m = mem2d.shape[0]
    return pl.pallas_call(
        _memkv_kernel,
        grid=(m // tm,),
        in_specs=[pl.BlockSpec((tm, D_MODEL), lambda i: (i, 0)),
                  pl.BlockSpec((D_MODEL, 2 * GROUP_WIDTH), lambda i: (0, 0))],
        out_specs=[pl.BlockSpec((tm, GROUP_WIDTH), lambda i: (i, 0))] * 2,
        out_shape=[jax.ShapeDtypeStruct((m, GROUP_WIDTH), F32)] * 2,
        compiler_params=_params("arbitrary"),
        name="memkv",
    )(mem2d, w_bf16)


def _causal_conv_chunk(u_ref, ext_ref, cw_ref, cb_ref, *, rows, chunk, multi_chunk):
    u = u_ref[0]
    ext_ref[SUBLANES:SUBLANES + rows, :] = u
    acc = cb_ref[...] + ext_ref[SUBLANES:SUBLANES + chunk, :] * cw_ref[CONV_W - 1:CONV_W, :]
    for i in range(CONV_W - 1):
        lo = SUBLANES - (CONV_W - 1) + i
        acc = acc + ext_ref[lo:lo + chunk, :] * cw_ref[i:i + 1, :]
    if multi_chunk:
        ext_ref[0:SUBLANES, :] = u[rows - SUBLANES:rows, :]
    return _silu(acc)


def _pad_rows(a, chunk):
    rows = a.shape[0]
    if rows == chunk:
        return a
    return jnp.concatenate([a, jnp.zeros((chunk - rows, a.shape[1]), a.dtype)], axis=0)


def _transpose_rows(a):
    n = a.shape[0]
    if n < LANES:
        a = jnp.concatenate([a, jnp.zeros((LANES - n, LANES), a.dtype)], axis=0)
    return a.T


def _decay_matrix(gcol, grow, tril):
    diff = gcol - grow
    return jnp.where(tril, jnp.exp(jnp.where(tril, diff, 0.0)), 0.0)


def _unit_lower_inverse(a, eye, blk):
    n = a.shape[0]
    abd = jnp.where(blk, a, 0.0)
    t = eye - abd
    p = abd
    steps = int(np.log2(INV_BLOCK)) - 1
    for _ in range(steps):
        p = _dot3(p, p)
        t = t + _dot3(t, p)
    if n > INV_BLOCK:
        assert n // INV_BLOCK <= 4
        e = jnp.where(blk, 0.0, a)
        nm = _dot3(t, e)
        n2 = _dot3(nm, nm)
        m = eye - nm
        m = m + _dot3(m, n2)
        t = _dot3(m, t)
    return t


def _gdn_kernel(u_ref, gate_ref, aux_ref, buf0_ref, s0_ref, cw_ref, cb_ref, nw_ref,
                y_ref, sout_ref, ext_ref, s_ref, *, rows, chunk, n_valid, n_chunks):
    n = pl.program_id(1)

    @pl.when(n == 0)
    def _():
        ext_ref[...] = jnp.zeros_like(ext_ref)
        ext_ref[0:SUBLANES, :] = buf0_ref[0]
        s_ref[...] = s0_ref[0]

    act = _causal_conv_chunk(u_ref, ext_ref, cw_ref, cb_ref, rows=rows, chunk=chunk, multi_chunk=n_chunks > 1)
    aux = _pad_rows(aux_ref[0], chunk)
    if n_valid < chunk:
        valid = _iota((chunk, 1), 0) < n_valid
        act = jnp.where(valid, act, 0.0)
        aux = jnp.where(valid, aux, 0.0)
    cum = _dot_exact_lhs(_tril_ones(chunk), aux)
    cum_t = _transpose_rows(cum)
    last = cum[chunk - 1:chunk, :]
    e_cum = jnp.exp(cum)
    e_tail = jnp.exp(last - cum)
    e_last = jnp.exp(last)
    ri = _iota((chunk, chunk), 0)
    ci = _iota((chunk, chunk), 1)
    tril = ci <= ri
    strict = ci < ri
    eye = jnp.where(ci == ri, 1.0, 0.0)
    blk = (ri // INV_BLOCK) == (ci // INV_BLOCK)
    gate = gate_ref[0]
    nw = nw_ref[...]
    for h in range(GDN_HEADS):
        q = act[:, h * GDN_DK:(h + 1) * GDN_DK]
        k = act[:, GROUP_WIDTH + h * GDN_DK:GROUP_WIDTH + (h + 1) * GDN_DK]
        v = act[:, 2 * GROUP_WIDTH + h * GDN_DK:2 * GROUP_WIDTH + (h + 1) * GDN_DK]
        qn = q * lax.rsqrt(jnp.sum(q * q, axis=-1, keepdims=True) + EPS) * GDN_DK ** -0.5
        kn = k * lax.rsqrt(jnp.sum(k * k, axis=-1, keepdims=True) + EPS)
        beta = aux[:, AUX_BETA + h:AUX_BETA + h + 1]
        gcol = cum[:, AUX_G + h:AUX_G + h + 1]
        grow = cum_t[AUX_G + h:AUX_G + h + 1, :chunk]
        decay = _decay_matrix(gcol, grow, tril)
        kb = kn.astype(BF16)
        kk = _dot_nt(kb, kb)
        a = jnp.where(strict, beta * kk * decay, 0.0)
        t_inv = _unit_lower_inverse(a, eye, blk)
        eg = e_cum[:, AUX_G + h:AUX_G + h + 1]
        rhs = jnp.concatenate([v * beta, kn * (beta * eg)], axis=1)
        sol = _dot3(t_inv, rhs)
        w_val = sol[:, :GDN_DK]
        w_key = sol[:, GDN_DK:]
        qk = _dot_nt(qn.astype(BF16), kb) * decay
        q_dec = qn * eg
        k_tail = kn * e_tail[:, AUX_G + h:AUX_G + h + 1]
        chunk_dec = e_last[:, AUX_G + h:AUX_G + h + 1]
        s = s_ref[h]
        sb = s.astype(BF16)
        u_new = w_val - _dot(w_key.astype(BF16), sb)
        ub = u_new.astype(BF16)
        o = _dot(q_dec.astype(BF16), sb) + _dot(qk.astype(BF16), ub)
        s_ref[h] = s * chunk_dec + _dot_tn(k_tail.astype(BF16), ub)
        o = o[:rows]
        on = o * lax.rsqrt(jnp.mean(o * o, axis=-1, keepdims=True) + EPS) * nw
        y_ref[0, :, h * GDN_DK:(h + 1) * GDN_DK] = (on * _silu(gate[:, h * GDN_DK:(h + 1) * GDN_DK])).astype(y_ref.dtype)

    @pl.when(n == n_chunks - 1)
    def _():
        sout_ref[0] = s_ref[...]


def _gdn(u, gate, aux, buf0, s0, cw, cb, nw, *, rows, n_valid, out_dtype):
    b, seq, _ = u.shape
    n_chunks = seq // rows
    kern = functools.partial(_gdn_kernel, rows=rows, chunk=CHUNK, n_valid=n_valid, n_chunks=n_chunks)
    return pl.pallas_call(
        kern,
        grid=(b, n_chunks),
        in_specs=[pl.BlockSpec((1, rows, GDN_QKV), lambda i, n: (i, n, 0)),
                  pl.BlockSpec((1, rows, GROUP_WIDTH), lambda i, n: (i, n, 0)),
                  pl.BlockSpec((1, rows, LANES), lambda i, n: (i, n, 0)),
                  pl.BlockSpec((1, SUBLANES, GDN_QKV), lambda i, n: (i, 0, 0)),
                  pl.BlockSpec((1, GDN_HEADS, GDN_DK, GDN_DK), lambda i, n: (i, 0, 0, 0)),
                  pl.BlockSpec((CONV_W, GDN_QKV), lambda i, n: (0, 0)),
                  pl.BlockSpec((1, GDN_QKV), lambda i, n: (0, 0)),
                  pl.BlockSpec((1, GDN_DK), lambda i, n: (0, 0))],
        out_specs=[pl.BlockSpec((1, rows, GROUP_WIDTH), lambda i, n: (i, n, 0)),
                   pl.BlockSpec((1, GDN_HEADS, GDN_DK, GDN_DK), lambda i, n: (i, 0, 0, 0))],
        out_shape=[jax.ShapeDtypeStruct((b, seq, GROUP_WIDTH), out_dtype),
                   jax.ShapeDtypeStruct((b, GDN_HEADS, GDN_DK, GDN_DK), F32)],
        scratch_shapes=[pltpu.VMEM((SUBLANES + CHUNK, GDN_QKV), F32),
                        pltpu.VMEM((GDN_HEADS, GDN_DK, GDN_DK), F32)],
        compiler_params=_params("arbitrary", "arbitrary"),
        name="gdn",
    )(u, gate, aux, buf0, s0, cw, cb, nw)


def _ssd_kernel(u_ref, gate_ref, aux_ref, buf0_ref, h0_ref, cw_ref, cb_ref, dvec_ref, nw_ref,
                y_ref, hout_ref, ext_ref, h_ref, *, rows, chunk, n_valid, n_chunks):
    n = pl.program_id(1)

    @pl.when(n == 0)
    def _():
        ext_ref[...] = jnp.zeros_like(ext_ref)
        ext_ref[0:SUBLANES, :] = buf0_ref[0]
        h_ref[...] = h0_ref[0]

    act = _causal_conv_chunk(u_ref, ext_ref, cw_ref, cb_ref, rows=rows, chunk=chunk, multi_chunk=n_chunks > 1)
    aux = _pad_rows(aux_ref[0], chunk)
    if n_valid < chunk:
        valid = _iota((chunk, 1), 0) < n_valid
        act = jnp.where(valid, act, 0.0)
        aux = jnp.where(valid, aux, 0.0)
    cum = _dot_exact_lhs(_tril_ones(chunk), aux)
    cum_t = _transpose_rows(cum)
    last = cum[chunk - 1:chunk, :]
    e_cum = jnp.exp(cum)
    e_tail = jnp.exp(last - cum)
    e_last = jnp.exp(last)
    ri = _iota((chunk, chunk), 0)
    ci = _iota((chunk, chunk), 1)
    tril = ci <= ri
    xs = act[:, :GROUP_WIDTH]
    heads_per_group = SSD_HEADS // SSD_GROUPS
    ys = []
    for g in range(SSD_GROUPS):
        bg = act[:, GROUP_WIDTH + g * SSD_N:GROUP_WIDTH + (g + 1) * SSD_N]
        cg = act[:, GROUP_WIDTH + SSD_GROUPS * SSD_N + g * SSD_N:GROUP_WIDTH + SSD_GROUPS * SSD_N + (g + 1) * SSD_N]
        cb_g = _dot_nt(cg.astype(BF16), bg.astype(BF16))
        for hh in range(heads_per_group):
            h = g * heads_per_group + hh
            dt = aux[:, AUX_DT + h:AUX_DT + h + 1]
            gcol = cum[:, AUX_DTA + h:AUX_DTA + h + 1]
            grow = cum_t[AUX_DTA + h:AUX_DTA + h + 1, :chunk]
            decay = _decay_matrix(gcol, grow, tril)
            x = xs[:, h * SSD_P:(h + 1) * SSD_P]
            xdt = x * dt
            xb = xdt.astype(BF16)
            y_intra = _dot((cb_g * decay).astype(BF16), xb)
            c_dec = cg * e_cum[:, AUX_DTA + h:AUX_DTA + h + 1]
            b_tail = bg * e_tail[:, AUX_DTA + h:AUX_DTA + h + 1]
            chunk_dec = e_last[:, AUX_DTA + h:AUX_DTA + h + 1]
            hs = h_ref[h]
            y = y_intra + _dot_nt(c_dec.astype(BF16), hs.astype(BF16))
            h_ref[h] = hs * chunk_dec + _dot_tn(xb, b_tail.astype(BF16))
            ys.append(y)
    y = jnp.concatenate(ys, axis=1) + dvec_ref[...] * xs
    y = y[:rows] * _silu(gate_ref[0])
    y = y * lax.rsqrt(jnp.mean(y * y, axis=-1, keepdims=True) + EPS) * nw_ref[...]
    y_ref[0] = y.astype(y_ref.dtype)

    @pl.when(n == n_chunks - 1)
    def _():
        hout_ref[0] = h_ref[...]


def _ssd(u, gate, aux, buf0, h0, cw, cb, dvec, nw, *, rows, n_valid, out_dtype):
    b, seq, _ = u.shape
    n_chunks = seq // rows
    kern = functools.partial(_ssd_kernel, rows=rows, chunk=CHUNK, n_valid=n_valid, n_chunks=n_chunks)
    return pl.pallas_call(
        kern,
        grid=(b, n_chunks),
        in_specs=[pl.BlockSpec((1, rows, SSD_XBC), lambda i, n: (i, n, 0)),
                  pl.BlockSpec((1, rows, GROUP_WIDTH), lambda i, n: (i, n, 0)),
                  pl.BlockSpec((1, rows, LANES), lambda i, n: (i, n, 0)),
                  pl.BlockSpec((1, SUBLANES, SSD_XBC), lambda i, n: (i, 0, 0)),
                  pl.BlockSpec((1, SSD_HEADS, SSD_P, SSD_N), lambda i, n: (i, 0, 0, 0)),
                  pl.BlockSpec((CONV_W, SSD_XBC), lambda i, n: (0, 0)),
                  pl.BlockSpec((1, SSD_XBC), lambda i, n: (0, 0)),
                  pl.BlockSpec((1, GROUP_WIDTH), lambda i, n: (0, 0)),
                  pl.BlockSpec((1, GROUP_WIDTH), lambda i, n: (0, 0))],
        out_specs=[pl.BlockSpec((1, rows, GROUP_WIDTH), lambda i, n: (i, n, 0)),
                   pl.BlockSpec((1, SSD_HEADS, SSD_P, SSD_N), lambda i, n: (i, 0, 0, 0))],
        out_shape=[jax.ShapeDtypeStruct((b, seq, GROUP_WIDTH), out_dtype),
                   jax.ShapeDtypeStruct((b, SSD_HEADS, SSD_P, SSD_N), F32)],
        scratch_shapes=[pltpu.VMEM((SUBLANES + CHUNK, SSD_XBC), F32),
                        pltpu.VMEM((SSD_HEADS, SSD_P, SSD_N), F32)],
        compiler_params=_params("arbitrary", "arbitrary"),
        name="ssd",
    )(u, gate, aux, buf0, h0, cw, cb, dvec, nw)


def _fox_kernel(q_ref, k_ref, v_ref, aux_ref, crow_ref, gate_ref, y_ref, m_ref, l_ref, acc_ref, *, tq, tk):
    qi = pl.program_id(1)
    ki = pl.program_id(2)

    @pl.when(ki == 0)
    def _():
        m_ref[...] = jnp.full_like(m_ref, NEG_BIG)
        l_ref[...] = jnp.zeros_like(l_ref)
        acc_ref[...] = jnp.zeros_like(acc_ref)

    @pl.when(ki <= qi)
    def _():
        causal = (ki * tk + _iota((tq, tk), 1)) <= (qi * tq + _iota((tq, tk), 0))
        aux = aux_ref[0]
        crow = crow_ref[0]
        lane_q = _iota((tq, LANES), 1)
        lane_k = _iota((tk, LANES), 1)
        for pair in range(FOX_HEADS // 2):
            sl = slice(pair * LANES, (pair + 1) * LANES)
            qp = q_ref[0, :, sl] * FOX_DH ** -0.5
            kp = k_ref[0, :, sl].astype(BF16)
            vp = v_ref[0, :, sl]
            pvs = []
            alphas = []
            for half in range(2):
                h = 2 * pair + half
                in_q = (lane_q // FOX_DH) == half
                in_k = (lane_k // FOX_DH) == half
                qm = jnp.where(in_q, qp, 0.0).astype(BF16)
                s = _dot_nt(qm, kp)
                s = s + aux[:, AUX_C + h:AUX_C + h + 1] - crow[h:h + 1, :]
                s = jnp.where(causal, s, NEG_BIG)
                m_old = m_ref[h]
                m_new = jnp.maximum(m_old, jnp.max(s, axis=-1, keepdims=True))
                alpha = jnp.exp(m_old - m_new)
                p = jnp.exp(s - m_new[:, 0:1])
                l_ref[h] = alpha * l_ref[h] + jnp.sum(p, axis=-1, keepdims=True)
                m_ref[h] = m_new
                vm = jnp.where(in_k, vp, 0.0).astype(BF16)
                pvs.append(_dot(p.astype(BF16), vm))
                alphas.append(alpha)
            alpha_pair = jnp.where((lane_q // FOX_DH) == 0, alphas[0], alphas[1])
            acc_ref[pair] = acc_ref[pair] * alpha_pair + (pvs[0] + pvs[1])

    @pl.when(ki == qi)
    def _():
        lane_q = _iota((tq, LANES), 1)
        for pair in range(FOX_HEADS // 2):
            sl = slice(pair * LANES, (pair + 1) * LANES)
            l_pair = jnp.where((lane_q // FOX_DH) == 0, l_ref[2 * pair], l_ref[2 * pair + 1])
            o = acc_ref[pair] / l_pair
            y_ref[0, :, sl] = (o * _silu(gate_ref[0, :, sl])).astype(y_ref.dtype)


def _fox_prompt(q, k, v, aux, crow, gate, *, tq=256, tk=256):
    b, seq, _ = q.shape
    assert tq == tk and seq % tq == 0
    nq = seq // tq
    kmap = lambda i, qi, ki: (i, jnp.minimum(ki, qi), 0)
    return pl.pallas_call(
        functools.partial(_fox_kernel, tq=tq, tk=tk),
        grid=(b, nq, nq),
        in_specs=[pl.BlockSpec((1, tq, GROUP_WIDTH), lambda i, qi, ki: (i, qi, 0)),
                  pl.BlockSpec((1, tk, GROUP_WIDTH), kmap),
                  pl.BlockSpec((1, tk, GROUP_WIDTH), kmap),
                  pl.BlockSpec((1, tq, LANES), lambda i, qi, ki: (i, qi, 0)),
                  pl.BlockSpec((1, FOX_HEADS, tk), lambda i, qi, ki: (i, 0, jnp.minimum(ki, qi))),
                  pl.BlockSpec((1, tq, GROUP_WIDTH), lambda i, qi, ki: (i, qi, 0))],
        out_specs=pl.BlockSpec((1, tq, GROUP_WIDTH), lambda i, qi, ki: (i, qi, 0)),
        out_shape=jax.ShapeDtypeStruct((b, seq, GROUP_WIDTH), BF16),
        scratch_shapes=[pltpu.VMEM((FOX_HEADS, tq, LANES), F32),
                        pltpu.VMEM((FOX_HEADS, tq, LANES), F32),
                        pltpu.VMEM((FOX_HEADS // 2, tq, LANES), F32)],
        compiler_params=_params("arbitrary", "arbitrary", "arbitrary"),
        name="fox_prompt",
    )(q, k, v, aux, crow, gate)


def _fox_decode_kernel(pt_ref, q_ref, knew_ref, vnew_ref, lfnew_ref, gate_ref, *refs, pages_per_step, n_steps, n_new):
    g_ = pages_per_step
    k_refs = refs[:g_]
    v_refs = refs[g_:2 * g_]
    lf_refs = refs[2 * g_:3 * g_]
    y_ref = refs[3 * g_]
    qb_ref, m_ref, l_ref, acc_ref, carry_ref, kpad_ref, vpad_ref = refs[3 * g_ + 1:]
    del pt_ref
    j = pl.program_id(1)
    nrow = n_new * FOX_HEADS
    page = k_refs[0].shape[0]
    head_of_lane = _iota((FOX_HEADS, GROUP_WIDTH), 1) // FOX_DH
    head_mask = head_of_lane == _iota((FOX_HEADS, GROUP_WIDTH), 0)
    triu = jnp.where(_iota((page, page), 0) <= _iota((page, page), 1), 1.0, 0.0).astype(BF16)

    def tile_heads(a):
        return jnp.concatenate([a] * n_new, axis=0)

    @pl.when(j == 0)
    def _():
        q = q_ref[0] * FOX_DH ** -0.5
        rows = [jnp.where(head_mask, jnp.broadcast_to(q[i:i + 1, :], (FOX_HEADS, GROUP_WIDTH)), 0.0)
                for i in range(n_new)]
        qb_ref[...] = jnp.concatenate(rows, axis=0)
        m_ref[...] = jnp.full_like(m_ref, NEG_BIG)
        l_ref[...] = jnp.zeros_like(l_ref)
        acc_ref[...] = jnp.zeros_like(acc_ref)
        carry_ref[...] = jnp.zeros_like(carry_ref)
        kpad_ref[...] = jnp.zeros_like(kpad_ref)
        vpad_ref[...] = jnp.zeros_like(vpad_ref)
        kpad_ref[0:SUBLANES, :] = knew_ref[0]
        vpad_ref[0:SUBLANES, :] = vnew_ref[0]

    qb = qb_ref[...].astype(BF16)

    def online_update(s_list, v_list):
        s_all = jnp.concatenate(s_list, axis=1) if len(s_list) > 1 else s_list[0]
        m_old = m_ref[...]
        m_new = jnp.maximum(m_old, jnp.max(s_all, axis=-1, keepdims=True))
        alpha = jnp.exp(m_old - m_new)
        p = jnp.exp(s_all - m_new[:, 0:1])
        l_ref[...] = alpha * l_ref[...] + jnp.sum(p, axis=-1, keepdims=True)
        m_ref[...] = m_new
        pv = None
        for idx, vv in enumerate(v_list):
            t = _dot(p[:, idx * page:(idx + 1) * page].astype(BF16), vv)
            pv = t if pv is None else pv + t
        acc_ref[...] = acc_ref[...] * alpha[:, 0:1] + pv

    carry = carry_ref[...]
    s_list, v_list = [], []
    for g in range(g_):
        cum = _dot_exact_rhs(lf_refs[g][...], triu)
        ck = carry + cum
        carry = carry + jnp.broadcast_to(cum[:, page - 1:page], carry.shape)
        s = _dot_nt(qb, k_refs[g][...].astype(BF16)) - tile_heads(ck)
        s_list.append(s)
        v_list.append(v_refs[g][...].astype(BF16))
    carry_ref[...] = carry
    online_update(s_list, v_list)

    @pl.when(j == n_steps - 1)
    def _():
        tot = carry_ref[...]
        cum = _dot_exact_rhs(lfnew_ref[0], triu)
        ck_new = tile_heads(tot + cum)
        row_q = _iota((nrow, page), 0) // FOX_HEADS
        lane = _iota((nrow, page), 1)
        cq = jnp.sum(jnp.where(lane == row_q, ck_new, 0.0), axis=-1, keepdims=True)
        m_ref[...] = m_ref[...] + cq
        s = _dot_nt(qb, kpad_ref[...].astype(BF16)) + cq - ck_new
        s = jnp.where(lane <= row_q, s, NEG_BIG)
        online_update([s], [vpad_ref[...].astype(BF16)])
        o = acc_ref[...] / l_ref[:, 0:1]
        out_row = _iota((SAMPLE_ROWS, GROUP_WIDTH), 0)
        out = jnp.zeros((SAMPLE_ROWS, GROUP_WIDTH), F32)
        for i in range(n_new):
            oi = jnp.where(head_mask, o[i * FOX_HEADS:(i + 1) * FOX_HEADS, :], 0.0)
            oi = jnp.sum(oi, axis=0, keepdims=True)
            out = jnp.where(out_row == i, jnp.broadcast_to(oi, out.shape), out)
        y_ref[0] = out * _silu(gate_ref[0])


def _fox_decode(page_table, q, knew, vnew, lfnew_t, gate, cache_k, cache_v, cache_lf_t, *, layer, n_new, pages_per_step=8):
    b = q.shape[0]
    n_pages = page_table.shape[1]
    page = cache_k.shape[2]
    g_ = pages_per_step
    assert n_pages % g_ == 0 and page == LANES
    n_steps = n_pages // g_
    nrow = n_new * FOX_HEADS

    def page_spec(shape_tail, g):
        return pl.BlockSpec((None, None) + shape_tail,
                            lambda i, j, pt, g=g: (layer, pt[i, j * g_ + g], 0, 0))

    row_spec = lambda w: pl.BlockSpec((1, SAMPLE_ROWS, w), lambda i, j, pt: (i, 0, 0))
    in_specs = ([row_spec(GROUP_WIDTH), row_spec(GROUP_WIDTH), row_spec(GROUP_WIDTH),
                 pl.BlockSpec((1, FOX_HEADS, page), lambda i, j, pt: (i, 0, 0)), row_spec(GROUP_WIDTH)]
                + [page_spec((page, GROUP_WIDTH), g) for g in range(g_)]
                + [page_spec((page, GROUP_WIDTH), g) for g in range(g_)]
                + [page_spec((FOX_HEADS, page), g) for g in range(g_)])
    grid_spec = pltpu.PrefetchScalarGridSpec(
        num_scalar_prefetch=1,
        grid=(b, n_steps),
        in_specs=in_specs,
        out_specs=pl.BlockSpec((1, SAMPLE_ROWS, GROUP_WIDTH), lambda i, j, pt: (i, 0, 0)),
        scratch_shapes=[pltpu.VMEM((nrow, GROUP_WIDTH), F32),
                        pltpu.VMEM((nrow, LANES), F32),
                        pltpu.VMEM((nrow, LANES), F32),
                        pltpu.VMEM((nrow, GROUP_WIDTH), F32),
                        pltpu.VMEM((FOX_HEADS, LANES), F32),
                        pltpu.VMEM((page, GROUP_WIDTH), F32),
                        pltpu.VMEM((page, GROUP_WIDTH), F32)])
    kern = functools.partial(_fox_decode_kernel, pages_per_step=g_, n_steps=n_steps, n_new=n_new)
    return pl.pallas_call(
        kern,
        grid_spec=grid_spec,
        out_shape=jax.ShapeDtypeStruct((b, SAMPLE_ROWS, GROUP_WIDTH), F32),
        compiler_params=_params("arbitrary", "arbitrary"),
        name="fox_decode",
    )(page_table, q, knew, vnew, lfnew_t, gate, *([cache_k] * g_), *([cache_v] * g_), *([cache_lf_t] * g_))


def _memattn_kernel(q_ref, gate_ref, mk_ref, mv_ref, y_ref):
    for h in range(MEM_HEADS):
        sl = slice(h * MEM_DH, (h + 1) * MEM_DH)
        s = _dot_nt(q_ref[0, :, sl].astype(BF16), mk_ref[0, :, sl].astype(BF16)) * MEM_DH ** -0.5
        m = jnp.max(s, axis=-1, keepdims=True)
        e = jnp.exp(s - m)
        p = e / jnp.sum(e, axis=-1, keepdims=True)
        o = _dot(p.astype(BF16), mv_ref[0, :, sl].astype(BF16))
        y_ref[0, :, sl] = (o * _silu(gate_ref[0, :, sl])).astype(y_ref.dtype)


def _memattn(q, gate, mk, mv, *, tq, out_dtype):
    b, seq, _ = q.shape
    mem_len = mk.shape[1]
    return pl.pallas_call(
        _memattn_kernel,
        grid=(b, seq // tq),
        in_specs=[pl.BlockSpec((1, tq, GROUP_WIDTH), lambda i, n: (i, n, 0)),
                  pl.BlockSpec((1, tq, GROUP_WIDTH), lambda i, n: (i, n, 0)),
                  pl.BlockSpec((1, mem_len, GROUP_WIDTH), lambda i, n: (i, 0, 0)),
                  pl.BlockSpec((1, mem_len, GROUP_WIDTH), lambda i, n: (i, 0, 0))],
        out_specs=pl.BlockSpec((1, tq, GROUP_WIDTH), lambda i, n: (i, n, 0)),
        out_shape=jax.ShapeDtypeStruct((b, seq, GROUP_WIDTH), out_dtype),
        compiler_params=_params("arbitrary", "arbitrary"),
        name="memattn",
    )(q, gate, mk, mv)


def _outproj_kernel(ya_ref, yb_ref, yc_ref, yd_ref, x_ref, w_ref, g_ref, b_ref, o_ref, *, alpha):
    y = None
    for idx, ref in enumerate((ya_ref, yb_ref, yc_ref, yd_ref)):
        t = _dot(ref[...].astype(BF16), w_ref[idx * GROUP_WIDTH:(idx + 1) * GROUP_WIDTH, :])
        y = t if y is None else y + t
    z = alpha * x_ref[...] + y
    mu = jnp.mean(z, axis=-1, keepdims=True)
    zc = z - mu
    var = jnp.mean(zc * zc, axis=-1, keepdims=True)
    o_ref[...] = zc * lax.rsqrt(var + EPS) * g_ref[...] + b_ref[...]


def _outproj(ya, yb, yc, yd, x2d, w_bf16, ln_g, ln_b, *, tm, alpha):
    m = x2d.shape[0]
    yspec = pl.BlockSpec((tm, GROUP_WIDTH), lambda i: (i, 0))
    return pl.pallas_call(
        functools.partial(_outproj_kernel, alpha=alpha),
        grid=(m // tm,),
        in_specs=[yspec, yspec, yspec, yspec,
                  pl.BlockSpec((tm, D_MODEL), lambda i: (i, 0)),
                  pl.BlockSpec((MIX_WIDTH, D_MODEL), lambda i: (0, 0)),
                  pl.BlockSpec((1, D_MODEL), lambda i: (0, 0)),
                  pl.BlockSpec((1, D_MODEL), lambda i: (0, 0))],
        out_specs=pl.BlockSpec((tm, D_MODEL), lambda i: (i, 0)),
        out_shape=jax.ShapeDtypeStruct((m, D_MODEL), F32),
        compiler_params=_params("arbitrary"),
        name="outproj",
    )(ya, yb, yc, yd, x2d, w_bf16, ln_g, ln_b)


def _conv_buf8(buf):
    return jnp.pad(buf, ((0, 0), (SUBLANES - (CONV_W - 1), 0), (0, 0)))


def _layer(x2d, *, batch, seq_rows, n_valid, rows, tm, alpha, prm, mem_k, mem_v, gdn_buf, gdn_s0, ssd_buf, ssd_h0,
           fox_fn, y_dtype):
    (a_qkv, a_gate, b_xbc, b_gate, c_q, c_k, c_v, c_gate, d_q, d_gate, aux) = _inproj(
        x2d, prm["w_packed"], prm["pvec"], tm=tm, seq_rows=seq_rows)
    r3 = lambda t: t.reshape(batch, seq_rows, t.shape[-1])
    a_qkv3, b_xbc3, aux3 = r3(a_qkv), r3(b_xbc), r3(aux)
    y_a, gdn_s = _gdn(a_qkv3, r3(a_gate), aux3, _conv_buf8(gdn_buf), gdn_s0,
                      prm["gdn_conv_w"], prm["gdn_conv_b"], prm["gdn_norm_w"],
                      rows=rows, n_valid=n_valid, out_dtype=y_dtype)
    y_b, ssd_h = _ssd(b_xbc3, r3(b_gate), aux3, _conv_buf8(ssd_buf), ssd_h0,
                      prm["ssd_conv_w"], prm["ssd_conv_b"], prm["ssd_dvec"], prm["ssd_norm_w"],
                      rows=rows, n_valid=n_valid, out_dtype=y_dtype)
    y_c = fox_fn(r3(c_q), r3(c_k), r3(c_v), aux3, r3(c_gate))
    y_d = _memattn(r3(d_q), r3(d_gate), mem_k, mem_v, tq=min(seq_rows, 256), out_dtype=y_dtype)
    f2 = lambda t: t.reshape(batch * seq_rows, GROUP_WIDTH)
    x_new = _outproj(f2(y_a), f2(y_b), f2(y_c), f2(y_d), x2d, prm["w_out"], prm["ln_g"], prm["ln_b"],
                     tm=tm, alpha=alpha)
    lo = n_valid - (CONV_W - 1)
    states = dict(gdn_conv=a_qkv3[:, lo:n_valid], gdn_state=gdn_s, ssd_conv=b_xbc3[:, lo:n_valid], ssd_state=ssd_h,
                  fox_k=r3(c_k)[:, :n_valid].reshape(batch, n_valid, FOX_HEADS, FOX_DH),
                  fox_v=r3(c_v)[:, :n_valid].reshape(batch, n_valid, FOX_HEADS, FOX_DH),
                  fox_logf=aux3[:, :n_valid, AUX_LOGF:AUX_LOGF + FOX_HEADS])
    return x_new, states


def kernel(x_prompt, x_sample, state_gdn_conv, state_gdn, state_ssd_conv, state_ssd, cache_fox_k, cache_fox_v, cache_fox_logf, cache_mem_k, cache_mem_v, page_table, mem_prompt, w_in, gdn_conv_w, gdn_conv_b, gdn_a_log, gdn_dt_bias, gdn_norm_w, ssd_conv_w, ssd_conv_b, ssd_a_log, ssd_dt_bias, ssd_d, ssd_norm_w, fox_f_bias, w_mem_kv, w_out, ln_g, ln_b):
    depth = w_in.shape[0]
    bp, lp, _ = x_prompt.shape
    bs, ls, _ = x_sample.shape
    mem_len = mem_prompt.shape[1]
    alpha = (2 * depth) ** 0.25
    n_pool, page = cache_fox_k.shape[1], cache_fox_k.shape[2]
    assert ls <= SAMPLE_ROWS and lp % CHUNK == 0

    cache_k4 = cache_fox_k.reshape(depth, n_pool, page, GROUP_WIDTH)
    cache_v4 = cache_fox_v.reshape(depth, n_pool, page, GROUP_WIDTH)
    cache_lf_t = jnp.swapaxes(cache_fox_logf, 2, 3)
    xp = x_prompt.reshape(bp * lp, D_MODEL)
    xs = jnp.pad(x_sample, ((0, 0), (0, SAMPLE_ROWS - ls), (0, 0))).reshape(bs * SAMPLE_ROWS, D_MODEL)
    mem2d = mem_prompt.reshape(bp * mem_len, D_MODEL)

    names = ("gdn_conv", "gdn_state", "ssd_conv", "ssd_state", "fox_k", "fox_v", "fox_logf")
    pn = {n: [] for n in names + ("mem_k", "mem_v")}
    sn = {n: [] for n in names}
    for l in range(depth):
        w_packed, pvec = _pack_inproj_weights(w_in[l], gdn_a_log[l], gdn_dt_bias[l], ssd_a_log[l], ssd_dt_bias[l],
                                              fox_f_bias[l])
        prm = dict(w_packed=w_packed, pvec=pvec,
                   gdn_conv_w=gdn_conv_w[l], gdn_conv_b=gdn_conv_b[l][None, :], gdn_norm_w=gdn_norm_w[l][None, :],
                   ssd_conv_w=ssd_conv_w[l], ssd_conv_b=ssd_conv_b[l][None, :],
                   ssd_dvec=jnp.repeat(ssd_d[l], SSD_P)[None, :], ssd_norm_w=ssd_norm_w[l][None, :],
                   w_out=w_out[l].astype(BF16), ln_g=ln_g[l][None, :], ln_b=ln_b[l][None, :])
        mk, mv = _memkv(mem2d, w_mem_kv[l].astype(BF16))
        mk3 = mk.reshape(bp, mem_len, GROUP_WIDTH)
        mv3 = mv.reshape(bp, mem_len, GROUP_WIDTH)

        def fox_p(q, k, v, aux3, gate):
            crow = jnp.swapaxes(aux3[:, :, AUX_C:AUX_C + FOX_HEADS], 1, 2)
            return _fox_prompt(q, k, v, aux3, crow, gate)

        xp, st = _layer(xp, batch=bp, seq_rows=lp, n_valid=lp, rows=CHUNK, tm=256, alpha=alpha, prm=prm,
                        mem_k=mk3, mem_v=mv3,
                        gdn_buf=jnp.zeros((bp, CONV_W - 1, GDN_QKV), F32),
                        gdn_s0=jnp.zeros((bp, GDN_HEADS, GDN_DK, GDN_DK), F32),
                        ssd_buf=jnp.zeros((bp, CONV_W - 1, SSD_XBC), F32),
                        ssd_h0=jnp.zeros((bp, SSD_HEADS, SSD_P, SSD_N), F32),
                        fox_fn=fox_p, y_dtype=BF16)
        for n in names:
            pn[n].append(st[n])
        pn["mem_k"].append(mk3.reshape(bp, mem_len, MEM_HEADS, MEM_DH))
        pn["mem_v"].append(mv3.reshape(bp, mem_len, MEM_HEADS, MEM_DH))

        def fox_s(q, k, v, aux3, gate, l=l):
            lf_t = jnp.swapaxes(aux3[:, :ls, AUX_LOGF:AUX_LOGF + FOX_HEADS], 1, 2)
            lf_t = jnp.pad(lf_t, ((0, 0), (0, 0), (0, page - ls)))
            return _fox_decode(page_table, q, k, v, lf_t, gate, cache_k4, cache_v4, cache_lf_t, layer=l, n_new=ls)

        xs, st = _layer(xs, batch=bs, seq_rows=SAMPLE_ROWS, n_valid=ls, rows=SAMPLE_ROWS, tm=bs * SAMPLE_ROWS,
                        alpha=alpha, prm=prm,
                        mem_k=cache_mem_k[l].reshape(bs, mem_len, GROUP_WIDTH),
                        mem_v=cache_mem_v[l].reshape(bs, mem_len, GROUP_WIDTH),
                        gdn_buf=state_gdn_conv[l], gdn_s0=state_gdn[l],
                        ssd_buf=state_ssd_conv[l], ssd_h0=state_ssd[l],
                        fox_fn=fox_s, y_dtype=F32)
        for n in names:
            sn[n].append(st[n])

    P = {n: jnp.stack(v) for n, v in pn.items()}
    S = {n: jnp.stack(v) for n, v in sn.items()}
    y_prompt = xp.reshape(bp, lp, D_MODEL)
    y_sample = xs.reshape(bs, SAMPLE_ROWS, D_MODEL)[:, :ls]
    return (y_prompt, y_sample,
            P["gdn_conv"], P["gdn_state"], P["ssd_conv"], P["ssd_state"],
            P["fox_k"], P["fox_v"], P["fox_logf"], P["mem_k"], P["mem_v"],
            S["gdn_conv"], S["gdn_state"], S["ssd_conv"], S["ssd_state"],
            S["fox_k"], S["fox_v"], S["fox_logf"])
```
